```python
import jax, jax.numpy as jnp
from jax import lax
import numpy as np

D_MODEL = 1024
BATCH = 8
SEQ = 8192
DEPTH = 2

CHUNK = 64
HEAD_DIM = 64
EPS = 1e-6
SWA_HEADS = 8
SWA_KV_HEADS = 2
SWA_WINDOW = 128
SWA_PREV_CHUNKS = SWA_WINDOW // CHUNK
GLA_HEADS = 4
GLA_DK = D_MODEL // 2 // GLA_HEADS
GLA_DV = D_MODEL // GLA_HEADS
GLA_RANK = 16
GLA_TAU = 16.0
REL_HEADS = 8
REL_PREV_CHUNKS = 8
REL_MAX_DIST = 128
SWA_WIDTH = SWA_HEADS * HEAD_DIM
SWA_KV_WIDTH = SWA_KV_HEADS * HEAD_DIM
GLA_K_WIDTH = GLA_HEADS * GLA_DK
GLA_V_WIDTH = GLA_HEADS * GLA_DV
REL_WIDTH = REL_HEADS * HEAD_DIM
MIX_WIDTH = SWA_WIDTH + GLA_V_WIDTH + REL_WIDTH
IN_SPLITS = (SWA_WIDTH, SWA_KV_WIDTH, SWA_KV_WIDTH,
             GLA_K_WIDTH, GLA_K_WIDTH, GLA_V_WIDTH, GLA_V_WIDTH, GLA_RANK,
             REL_WIDTH, REL_WIDTH, REL_WIDTH,
             D_MODEL, D_MODEL, D_MODEL)
IN_WIDTH = sum(IN_SPLITS)
IN_SPLIT_POINTS = tuple(int(p) for p in np.cumsum(IN_SPLITS)[:-1])
FFN_DIM = 7 * D_MODEL // 2
N_EXPERTS = 8
TOP_K = 2
MOE_BLOCK = 512
N_DENSE = (DEPTH + 1) // 2
N_MOE = DEPTH // 2

kernel_name = "hybrid_chunk_causal_gated_mixer_moe"


def rms_norm(x, g):
    xf = x.astype(jnp.float32)
    y = xf * lax.rsqrt(jnp.mean(xf * xf, axis=-1, keepdims=True) + EPS)
    return y.astype(x.dtype) * g


def band_rel_dist(n_prev):
    i = jnp.arange(CHUNK)[:, None]
    j = jnp.arange((n_prev + 1) * CHUNK)[None, :]
    return n_prev * CHUNK + i - j


def banded_chunk_attention(q, k, v, n_prev, bias, sinks):
    b, s, h, dh = q.shape
    kvh = k.shape[2]
    grp = h // kvh
    nc = s // CHUNK
    pad = n_prev * CHUNK
    band = pad + CHUNK
    kp = jnp.pad(k, ((0, 0), (pad, 0), (0, 0), (0, 0)))
    vp = jnp.pad(v, ((0, 0), (pad, 0), (0, 0), (0, 0)))
    qc = jnp.moveaxis(q.reshape(b, nc, CHUNK, kvh, grp, dh), 1, 0)
    bias = bias.astype(jnp.float32).reshape(kvh, grp, CHUNK, band)
    key_idx = jnp.arange(band)
    scale = dh ** -0.5

    def one_chunk(args):
        c, q_blk = args
        start = c * CHUNK
        k_blk = lax.dynamic_slice_in_dim(kp, start, band, axis=1)
        v_blk = lax.dynamic_slice_in_dim(vp, start, band, axis=1)
        sc = jnp.einsum('bqkgd,bskd->bkgqs', q_blk, k_blk).astype(jnp.float32) * scale + bias
        sc = jnp.where(start + key_idx >= pad, sc, -jnp.inf)
        if sinks is None:
            p = jax.nn.softmax(sc, axis=-1)
        else:
            sink = sinks.astype(jnp.float32).reshape(kvh, grp)[None, :, :, None, None]
            m = jnp.maximum(jnp.max(sc, axis=-1, keepdims=True), sink)
            e = jnp.exp(sc - m)
            p = e / (jnp.sum(e, axis=-1, keepdims=True) + jnp.exp(sink - m))
        o = jnp.einsum('bkgqs,bskd->bqkgd', p.astype(v.dtype), v_blk)
        return o.reshape(b, CHUNK, h * dh)

    out = lax.map(one_chunk, (jnp.arange(nc), qc))
    return jnp.moveaxis(out, 0, 1).reshape(b, s, h * dh)


def gla_attention(q, k, v, log_a):
    b, s, h, dk = q.shape
    dv = v.shape[-1]
    nc = s // CHUNK
    shp = lambda t: t.reshape(b, nc, CHUNK, h, t.shape[-1])
    q, k, v, log_a = shp(q), shp(k), shp(v), shp(log_a)
    g = jnp.cumsum(log_a, axis=2)
    g_last = g[:, :, -1:]
    q_dec = q * jnp.exp(g)
    k_inv = k * jnp.exp(-g)
    k_dec = k * jnp.exp(g_last - g)
    causal = jnp.tril(jnp.ones((CHUNK, CHUNK), dtype=bool))
    att = jnp.einsum('bcihd,bcjhd->bchij', q_dec, k_inv)
    att = jnp.where(causal, att, 0.0)
    o_intra = jnp.einsum('bchij,bcjhe->bcihe', att, v)
    decay = jnp.exp(g_last[:, :, 0])

    def step(state, inp):
        q_c, k_c, v_c, d_c = inp
        o_c = jnp.einsum('bihd,bhde->bihe', q_c, state)
        state = d_c[..., None] * state + jnp.einsum('bjhd,bjhe->bhde', k_c, v_c)
        return state, o_c

    xs = (jnp.moveaxis(q_dec, 1, 0), jnp.moveaxis(k_dec, 1, 0),
          jnp.moveaxis(v, 1, 0), jnp.moveaxis(decay, 1, 0))
    _, o_inter = lax.scan(step, jnp.zeros((b, h, dk, dv), q.dtype), xs)
    return (o_intra + jnp.moveaxis(o_inter, 0, 1)).reshape(b, s, h, dv)


def gla_branch(q, k, v, out_gate, gate_low, gate_up, gate_bias, norm_g):
    b, s, _ = q.shape
    f32 = jnp.float32
    split = lambda t, d: t.astype(f32).reshape(b, s, GLA_HEADS, d)
    log_a = jax.nn.log_sigmoid((gate_low @ gate_up + gate_bias).astype(f32)) / GLA_TAU
    o = gla_attention(split(q, GLA_DK) * (GLA_DK ** -0.5), split(k, GLA_DK),
                      split(v, GLA_DV), log_a.reshape(b, s, GLA_HEADS, GLA_DK))
    o = rms_norm(o, norm_g.astype(f32)) * jax.nn.silu(split(out_gate, GLA_DV))
    return o.reshape(b, s, GLA_V_WIDTH).astype(q.dtype)


def swiglu(x, w_gate, w_up, w_down):
    return (jax.nn.silu(x @ w_gate) * (x @ w_up)) @ w_down


def moe_swiglu(xf, router_w, w_gate, w_up, w_down):
    n, d = xf.shape
    n_assign = n * TOP_K
    logits = (xf @ router_w).astype(jnp.float32)
    top_logit, top_e = lax.top_k(logits, TOP_K)
    top_w = jax.nn.softmax(top_logit, axis=-1)
    flat_e = top_e.reshape(-1)
    order = jnp.argsort(flat_e, stable=True)
    sorted_e = flat_e[order]
    sorted_tok = order // TOP_K
    sorted_w = top_w.reshape(-1)[order]
    counts = jnp.bincount(flat_e, length=N_EXPERTS)
    padded = ((counts + MOE_BLOCK - 1) // MOE_BLOCK) * MOE_BLOCK
    starts = jnp.cumsum(counts) - counts
    pend = jnp.cumsum(padded)
    pstarts = pend - padded
    pos = jnp.arange(n_assign) - starts[sorted_e] + pstarts[sorted_e]
    n_blocks = -(-n_assign // MOE_BLOCK) + N_EXPERTS
    x_pad = jnp.zeros((n_blocks * MOE_BLOCK, d), xf.dtype).at[pos].set(xf[sorted_tok])
    block_start = jnp.arange(n_blocks) * MOE_BLOCK
    block_expert = jnp.minimum(jnp.sum(block_start[:, None] >= pend[None, :], axis=1), N_EXPERTS - 1)

    def expert_block(args):
        xb, e = args
        return swiglu(xb, w_gate[e], w_up[e], w_down[e])

    y_pad = lax.map(expert_block, (x_pad.reshape(n_blocks, MOE_BLOCK, d), block_expert))
    y_sorted = y_pad.reshape(-1, d)[pos] * sorted_w[:, None].astype(xf.dtype)
    return jnp.zeros((n, d), xf.dtype).at[sorted_tok].add(y_sorted)


def setup_inputs(seed: int = 0) -> dict:
    key = jax.random.key(seed)
    ks = jax.random.split(key, 24)
    f32 = jnp.float32
    nrm = lambda k, shape, scale: jax.random.normal(k, shape, f32) * scale
    gain = lambda k, shape: 1.0 + 0.02 * jax.random.normal(k, shape, f32)
    w_branch = jnp.concatenate([
        nrm(ks[8], (DEPTH, SWA_WIDTH, D_MODEL), SWA_WIDTH ** -0.5),
        nrm(ks[9], (DEPTH, GLA_V_WIDTH, D_MODEL), GLA_V_WIDTH ** -0.5),
        nrm(ks[10], (DEPTH, REL_WIDTH, D_MODEL), REL_WIDTH ** -0.5)], axis=1)
    return {
        "x": jax.random.normal(ks[0], (BATCH, SEQ, D_MODEL), f32),
        "mix_norm_g": gain(ks[1], (DEPTH, D_MODEL)),
        "w_in": nrm(ks[2], (DEPTH, D_MODEL, IN_WIDTH), D_MODEL ** -0.5),
        "attn_sinks": nrm(ks[3], (DEPTH, SWA_HEADS), 0.5),
        "gla_gate_up": nrm(ks[4], (DEPTH, GLA_RANK, GLA_K_WIDTH), GLA_RANK ** -0.5),
        "gla_gate_bias": nrm(ks[5], (DEPTH, GLA_K_WIDTH), 0.1),
        "gla_norm_g": gain(ks[6], (DEPTH, GLA_DV)),
        "rel_bias": nrm(ks[7], (DEPTH, REL_HEADS, 2 * REL_MAX_DIST + 1), 0.1),
        "w_branch": w_branch,
        "w_out": nrm(ks[11], (DEPTH, D_MODEL, D_MODEL), D_MODEL ** -0.5),
        "ffn_norm_g": gain(ks[12], (DEPTH, D_MODEL)),
        "dense_w_gate": nrm(ks[13], (N_DENSE, D_MODEL, FFN_DIM), D_MODEL ** -0.5),
        "dense_w_up": nrm(ks[14], (N_DENSE, D_MODEL, FFN_DIM), D_MODEL ** -0.5),
        "dense_w_down": nrm(ks[15], (N_DENSE, FFN_DIM, D_MODEL), FFN_DIM ** -0.5),
        "router_w": nrm(ks[16], (N_MOE, D_MODEL, N_EXPERTS), D_MODEL ** -0.5),
        "moe_w_gate": nrm(ks[17], (N_MOE, N_EXPERTS, D_MODEL, FFN_DIM), D_MODEL ** -0.5),
        "moe_w_up": nrm(ks[18], (N_MOE, N_EXPERTS, D_MODEL, FFN_DIM), D_MODEL ** -0.5),
        "moe_w_down": nrm(ks[19], (N_MOE, N_EXPERTS, FFN_DIM, D_MODEL), FFN_DIM ** -0.5),
        "final_norm_g": gain(ks[20], (D_MODEL,)),
    }


def reference(x, mix_norm_g, w_in, attn_sinks, gla_gate_up, gla_gate_bias, gla_norm_g,
              rel_bias, w_branch, w_out, ffn_norm_g, dense_w_gate, dense_w_up, dense_w_down,
              router_w, moe_w_gate, moe_w_up, moe_w_down, final_norm_g):
    b, s, _ = x.shape
    slopes = 2.0 ** (-8.0 * jnp.arange(1, SWA_HEADS + 1, dtype=jnp.float32) / SWA_HEADS)
    alibi = -slopes[:, None, None] * jnp.abs(band_rel_dist(SWA_PREV_CHUNKS)).astype(jnp.float32)
    rel_idx = jnp.clip(band_rel_dist(REL_PREV_CHUNKS), -REL_MAX_DIST, REL_MAX_DIST) + REL_MAX_DIST
    heads = lambda t, n: t.reshape(b, s, n, -1)
    for layer in range(DEPTH):
        xn = rms_norm(x, mix_norm_g[layer])
        proj = xn @ w_in[layer]
        (qa, ka, va, qb, kb, vb, ob, lb, qc, kc, vc, ga, gb, gc) = jnp.split(proj, IN_SPLIT_POINTS, axis=-1)
        o_a = banded_chunk_attention(heads(qa, SWA_HEADS), heads(ka, SWA_KV_HEADS), heads(va, SWA_KV_HEADS),
                                     SWA_PREV_CHUNKS, alibi, attn_sinks[layer])
        o_b = gla_branch(qb, kb, vb, ob, lb, gla_gate_up[layer], gla_gate_bias[layer], gla_norm_g[layer])
        o_c = banded_chunk_attention(heads(qc, REL_HEADS), heads(kc, REL_HEADS), heads(vc, REL_HEADS),
                                     REL_PREV_CHUNKS, rel_bias[layer][:, rel_idx], None)
        wb = w_branch[layer]
        y_a = o_a @ wb[:SWA_WIDTH]
        y_b = o_b @ wb[SWA_WIDTH:SWA_WIDTH + GLA_V_WIDTH]
        y_c = o_c @ wb[SWA_WIDTH + GLA_V_WIDTH:]
        merged = jax.nn.sigmoid(ga) * y_a + jax.nn.sigmoid(gb) * y_b + jax.nn.sigmoid(gc) * y_c
        x = x + merged @ w_out[layer]
        hn = rms_norm(x, ffn_norm_g[layer])
        i = layer // 2
        if layer % 2 == 0:
            f = swiglu(hn, dense_w_gate[i], dense_w_up[i], dense_w_down[i])
        else:
            f = moe_swiglu(hn.reshape(b * s, -1), router_w[i], moe_w_gate[i], moe_w_up[i],
                           moe_w_down[i]).reshape(b, s, -1)
        x = x + f
    return rms_norm(x, final_norm_g)
```

```python
import functools

import jax
import jax.numpy as jnp
import numpy as np
from jax import lax
from jax.experimental import pallas as pl
from jax.experimental.pallas import tpu as pltpu

F32 = jnp.float32
BF16 = jnp.bfloat16

D_MODEL = 1024
CHUNK = 64
HEAD_DIM = 64
EPS = 1e-6
SWA_HEADS = 8
SWA_KV_HEADS = 2
SWA_PREV_CHUNKS = 2
GLA_HEADS = 4
GLA_DK = 128
GLA_DV = 256
GLA_RANK = 16
GLA_TAU = 16.0
REL_HEADS = 8
REL_PREV_CHUNKS = 8
REL_MAX_DIST = 128
FFN_DIM = 3584
N_EXPERTS = 8
MOE_BLOCK = 512

VMEM_LIMIT_BYTES = 56 * 1024 * 1024
LANES = 128

_REF_SPLITS = dict(qa=(0, 512), ka=(512, 128), va=(640, 128), qb=(768, 512), kb=(1280, 512),
                   vb=(1792, 1024), ob=(2816, 1024), lb=(3840, 16), qc=(3856, 512),
                   kc=(4368, 512), vc=(4880, 512), ga=(5392, 1024), gb=(6416, 1024),
                   gc=(7440, 1024))
_COL = dict(ga=0, gb=1024, gc=2048, vb=3072, ob=4096, qa=5120, qb=5632, kb=6144, qc=6656,
            kc=7168, vc=7680, ka=8192, va=8320, lb=8448)
PROJ_WIDTH = 8704
PROJ_TILE_N = 512

_SWA_HEAD_ORDER = (0, 4, 1, 5, 2, 6, 3, 7)
NEG_BIG = -1e30


def _cparams(sem):
    return pltpu.CompilerParams(dimension_semantics=sem, vmem_limit_bytes=VMEM_LIMIT_BYTES)


def _sigmoid(x):
    return 1.0 / (1.0 + jnp.exp(-x))


def _inproj_kernel(x_ref, g_ref, w_ref, o_ref, xn_ref):
    @pl.when(pl.program_id(1) == 0)
    def _():
        x = x_ref[...]
        ms = jnp.mean(x * x, axis=-1, keepdims=True)
        xn_ref[...] = (x * lax.rsqrt(ms + EPS) * g_ref[...]).astype(BF16)

    o_ref[...] = jnp.dot(xn_ref[...], w_ref[...], preferred_element_type=F32).astype(o_ref.dtype)


def _inproj(x2, g, w, tm):
    n = x2.shape[0]
    return pl.pallas_call(
        _inproj_kernel,
        grid=(n // tm, PROJ_WIDTH // PROJ_TILE_N),
        in_specs=[pl.BlockSpec((tm, D_MODEL), lambda i, j: (i, 0)),
                  pl.BlockSpec((1, D_MODEL), lambda i, j: (0, 0)),
                  pl.BlockSpec((D_MODEL, PROJ_TILE_N), lambda i, j: (0, j))],
        out_specs=pl.BlockSpec((tm, PROJ_TILE_N), lambda i, j: (i, j)),
        out_shape=jax.ShapeDtypeStruct((n, PROJ_WIDTH), BF16),
        scratch_shapes=[pltpu.VMEM((tm, D_MODEL), BF16)],
        compiler_params=_cparams(("parallel", "arbitrary")),
        name="inproj",
    )(x2, g, w)


def _attn_kernel(*refs, n_pairs, seg_lens, kv_shared, has_sink):
    nseg = len(seg_lens)
    q_ref = refs[0]
    k_refs = refs[1:1 + nseg]
    v_refs = refs[1 + nseg:1 + 2 * nseg]
    bias_ref = refs[1 + 2 * nseg]
    sink_ref = refs[2 + 2 * nseg] if has_sink else None
    o_ref = refs[-1]

    lane = lax.broadcasted_iota(jnp.int32, (1, LANES), 1)
    lo = lane < HEAD_DIM
    offs = np.concatenate([[0], np.cumsum(seg_lens)]).astype(int)
    nt = (((1,), (1,)), ((), ()))
    for p in range(n_pairs):
        cols = slice(LANES * p, LANES * (p + 1))
        q2 = q_ref[:, cols]
        halves = []
        for hf in range(2):
            hh = 2 * p + hf
            qm = jnp.where(lo if hf == 0 else jnp.logical_not(lo), q2, jnp.zeros_like(q2))
            s = []
            for j in range(nseg):
                kj = k_refs[j][...] if kv_shared else k_refs[j][:, cols]
                sj = lax.dot_general(qm, kj, nt, preferred_element_type=F32)
                s.append(sj + bias_ref[0, hh, :, offs[j]:offs[j + 1]])
            m = s[0].max(axis=-1, keepdims=True)
            for j in range(1, nseg):
                m = jnp.maximum(m, s[j].max(axis=-1, keepdims=True))
            if has_sink:
                sink = sink_ref[hh:hh + 1, 0:1]
                m = jnp.maximum(m, sink)
                den = jnp.exp(sink - m)
            else:
                den = jnp.zeros_like(m)
            acc = None
            for j in range(nseg):
                e = jnp.exp(s[j] - m)
                den = den + e.sum(axis=-1, keepdims=True)
                vj = v_refs[j][...] if kv_shared else v_refs[j][:, cols]
                pv = jnp.dot(e.astype(BF16), vj, preferred_element_type=F32)
                acc = pv if acc is None else acc + pv
            halves.append(acc / den)
        o_ref[:, cols] = jnp.where(lo, halves[0], halves[1]).astype(o_ref.dtype)


def _banded_attention(proj, bias, sinks, *, q_col, k_col, v_col, n_heads, kv_shared, seg_lens,
                      tq, seq_len):
    n = proj.shape[0]
    nseg = len(seg_lens)
    n_pairs = n_heads // 2
    width = n_heads * HEAD_DIM
    kvw = LANES if kv_shared else width
    tiles_per_seq = seq_len // tq
    assert seg_lens[-1] == tq

    in_specs = [pl.BlockSpec((tq, width), lambda i: (i, q_col // width))]
    back = []
    tot = sum(seg_lens)
    for j, sl in enumerate(seg_lens):
        back.append(sum(seg_lens[j:]) - tq)
    for col in (k_col, v_col):
        for j, sl in enumerate(seg_lens):
            assert tq % sl == 0 and back[j] % sl == 0
            in_specs.append(pl.BlockSpec(
                (sl, kvw),
                functools.partial(lambda i, r, b, c: (jnp.maximum(i * r - b, 0), c),
                                  r=tq // sl, b=back[j] // sl, c=col // kvw)))
    in_specs.append(pl.BlockSpec(
        (1, n_heads, tq, tot),
        lambda i: (jnp.minimum(i % tiles_per_seq, nseg - 1), 0, 0, 0)))
    args = [proj] + [proj] * (2 * nseg) + [bias]
    if sinks is not None:
        in_specs.append(pl.BlockSpec((n_heads, LANES), lambda i: (0, 0)))
        args.append(sinks)
    kern = functools.partial(_attn_kernel, n_pairs=n_pairs, seg_lens=tuple(seg_lens),
                             kv_shared=kv_shared, has_sink=sinks is not None)
    return pl.pallas_call(
        kern,
        grid=(n // tq,),
        in_specs=in_specs,
        out_specs=pl.BlockSpec((tq, width), lambda i: (i, 0)),
        out_shape=jax.ShapeDtypeStruct((n, width), BF16),
        compiler_params=_cparams(("parallel",)),
        name="attn_swa" if kv_shared else "attn_rel",
    )(*args)


def _band_bias(base_fn, tq, seg_lens, n_prev):
    tot = sum(seg_lens)
    nseg = len(seg_lens)
    i = jnp.arange(tq)[:, None]
    j = jnp.arange(tot)[None, :] - (tot - tq)
    dist = i - j
    cq = i // CHUNK
    ck = jnp.floor_divide(j, CHUNK)
    allowed = (ck <= cq) & (ck >= cq - n_prev)
    base = base_fn(dist)
    seg_id = jnp.asarray(np.repeat(np.arange(nseg), seg_lens))[None, :]
    out = []
    for v in range(nseg):
        ok = allowed & (seg_id >= (nseg - 1 - v))
        out.append(jnp.where(ok[None], base, NEG_BIG))
    return jnp.stack(out).astype(F32)


def _gla_kernel(q_ref, k_ref, v_ref, og_ref, lb_ref, gu_ref, gbias_ref, ng_ref, ltri_ref,
                lfull_ref, o_ref, st_ref, *, tq):
    @pl.when(pl.program_id(1) == 0)
    def _():
        st_ref[...] = jnp.zeros_like(st_ref)

    n_chunks = tq // CHUNK
    z = jnp.dot(lb_ref[...], gu_ref[...], preferred_element_type=F32) + gbias_ref[...]
    log_a = (jnp.minimum(z, 0.0) - jnp.log1p(jnp.exp(-jnp.abs(z)))) * (1.0 / GLA_TAU)
    a_hi = log_a.astype(BF16)
    a_lo = (log_a - a_hi.astype(F32)).astype(BF16)
    ltri = ltri_ref[...]
    lfull = lfull_ref[...]
    g = (jnp.dot(ltri, a_hi, preferred_element_type=F32)
         + jnp.dot(ltri, a_lo, preferred_element_type=F32))
    g_last = (jnp.dot(lfull, a_hi, preferred_element_type=F32)
              + jnp.dot(lfull, a_lo, preferred_element_type=F32))
    q = q_ref[...].astype(F32)
    k = k_ref[...].astype(F32)
    q_dec = (q * ((GLA_DK ** -0.5) * jnp.exp(g))).astype(BF16)
    k_inv = (k * jnp.exp(-g)).astype(BF16)
    k_dec = (k * jnp.exp(g_last - g)).astype(BF16)
    decay = jnp.exp(g_last)

    row = lax.broadcasted_iota(jnp.int32, (tq, tq), 0)
    col = lax.broadcasted_iota(jnp.int32, (tq, tq), 1)
    shift = CHUNK.bit_length() - 1
    causal = (col <= row) & (jnp.right_shift(row, shift) == jnp.right_shift(col, shift))
    nt = (((1,), (1,)), ((), ()))
    tn = (((0,), (0,)), ((), ()))
    for h in range(GLA_HEADS):
        kc = slice(GLA_DK * h, GLA_DK * (h + 1))
        vc = slice(GLA_DV * h, GLA_DV * (h + 1))
        qd = q_dec[:, kc]
        v = v_ref[:, vc]
        att = lax.dot_general(qd, k_inv[:, kc], nt, preferred_element_type=F32)
        att = jnp.where(causal, att, 0.0)
        o_intra = jnp.dot(att.astype(BF16), v, preferred_element_type=F32)
        inter = []
        for c in range(n_chunks):
            rows = slice(CHUNK * c, CHUNK * (c + 1))
            st = st_ref[h]
            inter.append(lax.dot_general(qd[rows], st.astype(BF16), nt,
                                         preferred_element_type=F32))
            upd = lax.dot_general(v[rows], k_dec[rows, kc], tn, preferred_element_type=F32)
            st_ref[h] = decay[CHUNK * c:CHUNK * c + 1, kc] * st + upd
        o = o_intra + jnp.concatenate(inter, axis=0)
        ms = jnp.mean(o * o, axis=-1, keepdims=True)
        y = o * lax.rsqrt(ms + EPS) * ng_ref[...]
        gate = og_ref[:, vc].astype(F32)
        o_ref[:, vc] = (y * (gate * _sigmoid(gate))).astype(o_ref.dtype)


def _gla(proj, gate_up, gate_bias, norm_g, ltri, lfull, *, tq, batch, seq_len):
    n = proj.shape[0]
    tps = seq_len // tq
    kw = GLA_HEADS * GLA_DK
    vw = GLA_HEADS * GLA_DV
    row = lambda b, t: b * tps + t
    const = lambda b, t: (0, 0)
    return pl.pallas_call(
        functools.partial(_gla_kernel, tq=tq),
        grid=(batch, tps),
        in_specs=[pl.BlockSpec((tq, kw), lambda b, t: (row(b, t), _COL["qb"] // kw)),
                  pl.BlockSpec((tq, kw), lambda b, t: (row(b, t), _COL["kb"] // kw)),
                  pl.BlockSpec((tq, vw), lambda b, t: (row(b, t), _COL["vb"] // vw)),
                  pl.BlockSpec((tq, vw), lambda b, t: (row(b, t), _COL["ob"] // vw)),
                  pl.BlockSpec((tq, LANES), lambda b, t: (row(b, t), _COL["lb"] // LANES)),
                  pl.BlockSpec((LANES, kw), const),
                  pl.BlockSpec((1, kw), const),
                  pl.BlockSpec((1, GLA_DV), const),
                  pl.BlockSpec((tq, tq), const),
                  pl.BlockSpec((tq, tq), const)],
        out_specs=pl.BlockSpec((tq, vw), lambda b, t: (row(b, t), 0)),
        out_shape=jax.ShapeDtypeStruct((n, vw), BF16),
        scratch_shapes=[pltpu.VMEM((GLA_HEADS, GLA_DV, GLA_DK), F32)],
        compiler_params=_cparams(("parallel", "arbitrary")),
        name="gla",
    )(proj, proj, proj, proj, proj, gate_up, gate_bias, norm_g, ltri, lfull)


def _merge_kernel(*refs, moe):
    (ga_ref, gb_ref, gc_ref, oa_ref, ob_ref, oc_ref, x_ref, wa_ref, wb_ref, wc_ref, wo_ref,
     ng_ref) = refs[:12]
    if moe:
        rw_ref, xo_ref, hn_ref, lg_ref = refs[12:]
    else:
        xo_ref, hn_ref = refs[12:]
    ya = jnp.dot(oa_ref[...], wa_ref[...], preferred_element_type=F32)
    merged = _sigmoid(ga_ref[...].astype(F32)) * ya
    yb = jnp.dot(ob_ref[...], wb_ref[...], preferred_element_type=F32)
    merged = merged + _sigmoid(gb_ref[...].astype(F32)) * yb
    yc = jnp.dot(oc_ref[...], wc_ref[...], preferred_element_type=F32)
    merged = merged + _sigmoid(gc_ref[...].astype(F32)) * yc
    xn = x_ref[...] + jnp.dot(merged.astype(BF16), wo_ref[...], preferred_element_type=F32)
    xo_ref[...] = xn
    ms = jnp.mean(xn * xn, axis=-1, keepdims=True)
    h = xn * lax.rsqrt(ms + EPS) * ng_ref[...]
    hn_ref[...] = h.astype(hn_ref.dtype)
    if moe:
        lg_ref[...] = jnp.dot(h.astype(BF16), rw_ref[...], preferred_element_type=F32)


def _merge(proj, oa, ob, oc, x2, wa, wb, wc, wo, ng, router_w, *, tm):
    n = x2.shape[0]
    moe = router_w is not None
    rowb = lambda w: pl.BlockSpec((tm, w), lambda i: (i, 0))
    const = lambda s: pl.BlockSpec(s, lambda i: (0, 0))
    in_specs = [pl.BlockSpec((tm, D_MODEL), lambda i: (i, _COL["ga"] // D_MODEL)),
                pl.BlockSpec((tm, D_MODEL), lambda i: (i, _COL["gb"] // D_MODEL)),
                pl.BlockSpec((tm, D_MODEL), lambda i: (i, _COL["gc"] // D_MODEL)),
                rowb(oa.shape[1]), rowb(ob.shape[1]), rowb(oc.shape[1]), rowb(D_MODEL),
                const(wa.shape), const(wb.shape), const(wc.shape), const(wo.shape),
                const((1, D_MODEL))]
    args = [proj, proj, proj, oa, ob, oc, x2, wa, wb, wc, wo, ng]
    out_specs = [rowb(D_MODEL), rowb(D_MODEL)]
    out_shape = [jax.ShapeDtypeStruct((n, D_MODEL), F32),
                 jax.ShapeDtypeStruct((n, D_MODEL), F32 if moe else BF16)]
    if moe:
        in_specs.append(const((D_MODEL, LANES)))
        args.append(router_w)
        out_specs.append(rowb(LANES))
        out_shape.append(jax.ShapeDtypeStruct((n, LANES), F32))
    return pl.pallas_call(
        functools.partial(_merge_kernel, moe=moe),
        grid=(n // tm,),
        in_specs=in_specs,
        out_specs=out_specs,
        out_shape=out_shape,
        compiler_params=_cparams(("parallel",)),
        name="merge_moe" if moe else "merge",
    )(*args)


def _ffn_kernel(hn_ref, x_ref, wg_ref, wu_ref, wd_ref, o_ref):
    @pl.when(pl.program_id(1) == 0)
    def _():
        o_ref[...] = x_ref[...]

    h = hn_ref[...]
    a = jnp.dot(h, wg_ref[...], preferred_element_type=F32)
    b = jnp.dot(h, wu_ref[...], preferred_element_type=F32)
    act = (a * _sigmoid(a) * b).astype(BF16)
    o_ref[...] += jnp.dot(act, wd_ref[...], preferred_element_type=F32)


def _dense_ffn(hn, x2, wg, wu, wd, *, tm, tf):
    n = x2.shape[0]
    return pl.pallas_call(
        _ffn_kernel,
        grid=(n // tm, FFN_DIM // tf),
        in_specs=[pl.BlockSpec((tm, D_MODEL), lambda i, f: (i, 0)),
                  pl.BlockSpec((tm, D_MODEL), lambda i, f: (i, 0)),
                  pl.BlockSpec((D_MODEL, tf), lambda i, f: (0, f)),
                  pl.BlockSpec((D_MODEL, tf), lambda i, f: (0, f)),
                  pl.BlockSpec((tf, D_MODEL), lambda i, f: (f, 0))],
        out_specs=pl.BlockSpec((tm, D_MODEL), lambda i, f: (i, 0)),
        out_shape=jax.ShapeDtypeStruct((n, D_MODEL), F32),
        compiler_params=_cparams(("parallel", "arbitrary")),
        name="dense_ffn",
    )(hn, x2, wg, wu, wd)


def _route_kernel(lg_ref, ltri_ref, meta_ref, wts_ref, cnt_ref, run_ref):
    @pl.when(pl.program_id(0) == 0)
    def _():
        run_ref[...] = jnp.zeros_like(run_ref)

    shape = lg_ref.shape
    lane = lax.broadcasted_iota(jnp.int32, shape, 1).astype(F32)
    lg = jnp.where(lane < N_EXPERTS, lg_ref[...], -jnp.inf)
    m1 = lg.max(axis=-1, keepdims=True)
    i1 = jnp.where(lg == m1, lane, float(LANES)).min(axis=-1, keepdims=True)
    lg2 = jnp.where(lane == i1, -jnp.inf, lg)
    m2 = lg2.max(axis=-1, keepdims=True)
    i2 = jnp.where(lg2 == m2, lane, float(LANES)).min(axis=-1, keepdims=True)
    e = jnp.exp(m2 - m1)
    w1 = 1.0 / (1.0 + e)
    w2 = e / (1.0 + e)
    sel1 = lane == i1
    sel2 = lane == i2
    onehot = jnp.where(sel1 | sel2, 1.0, 0.0)
    before = jnp.dot(ltri_ref[...], onehot.astype(BF16), preferred_element_type=F32) + run_ref[...]
    r1 = jnp.where(sel1, before, 0.0).sum(axis=-1, keepdims=True)
    r2 = jnp.where(sel2, before, 0.0).sum(axis=-1, keepdims=True)
    run_ref[...] += onehot.sum(axis=0, keepdims=True)
    meta = jnp.where(lane == 0.0, i1, jnp.where(lane == 1.0, i2, jnp.where(
        lane == 2.0, r1, jnp.where(lane == 3.0, r2, 0.0))))
    meta_ref[...] = meta.astype(jnp.int32)
    wts_ref[...] = jnp.where(lane == 0.0, w1, jnp.where(lane == 1.0, w2, 0.0))
    cnt_ref[...] = run_ref[...]


def _route(logits, ltri, *, tm):
    n = logits.shape[0]
    rowb = pl.BlockSpec((tm, LANES), lambda i: (i, 0))
    return pl.pallas_call(
        _route_kernel,
        grid=(n // tm,),
        in_specs=[rowb, pl.BlockSpec((tm, tm), lambda i: (0, 0))],
        out_specs=[rowb, rowb, pl.BlockSpec((1, LANES), lambda i: (0, 0))],
        out_shape=[jax.ShapeDtypeStruct((n, LANES), jnp.int32),
                   jax.ShapeDtypeStruct((n, LANES), F32),
                   jax.ShapeDtypeStruct((1, LANES), F32)],
        scratch_shapes=[pltpu.VMEM((1, LANES), F32)],
        compiler_params=_cparams(("arbitrary",)),
        name="route",
    )(logits, ltri)


def _row_copy(src, src_row, dst, dst_row, sem):
    return pltpu.make_async_copy(src.at[pl.ds(src_row, 1)], dst.at[pl.ds(dst_row, 1)], sem)


def _dispatch_kernel(pos_hbm, hn_ref, xpad_in, xpad_out, pos_smem, sem_idx, sem_rows, *, tm):
    del xpad_in
    i = pl.program_id(0)
    idx_cp = pltpu.make_async_copy(pos_hbm.at[pl.ds(pl.multiple_of(i * (2 * tm), 2 * tm), 2 * tm)],
                                   pos_smem, sem_idx)
    idx_cp.start()
    idx_cp.wait()

    def issue(r, carry):
        _row_copy(hn_ref, r, xpad_out, pos_smem[2 * r], sem_rows).start()
        _row_copy(hn_ref, r, xpad_out, pos_smem[2 * r + 1], sem_rows).start()
        return carry

    lax.fori_loop(0, tm, issue, 0)

    def drain(r, carry):
        _row_copy(hn_ref, 0, xpad_out, 0, sem_rows).wait()
        _row_copy(hn_ref, 0, xpad_out, 0, sem_rows).wait()
        return carry

    lax.fori_loop(0, tm, drain, 0)


def _dispatch(pos_flat, hn32, xpad, *, tm):
    n = hn32.shape[0]
    return pl.pallas_call(
        functools.partial(_dispatch_kernel, tm=tm),
        grid=(n // tm,),
        in_specs=[pl.BlockSpec(memory_space=pl.ANY),
                  pl.BlockSpec((tm, D_MODEL), lambda i: (i, 0)),
                  pl.BlockSpec(memory_space=pl.ANY)],
        out_specs=pl.BlockSpec(memory_space=pl.ANY),
        out_shape=jax.ShapeDtypeStruct(xpad.shape, xpad.dtype),
        scratch_shapes=[pltpu.SMEM((2 * tm,), jnp.int32), pltpu.SemaphoreType.DMA,
                        pltpu.SemaphoreType.DMA],
        input_output_aliases={2: 0},
        compiler_params=_cparams(("arbitrary",)),
        name="dispatch",
    )(pos_flat, hn32, xpad)


def _expert_kernel(be_ref, nu_ref, x_ref, wg_ref, wu_ref, wd_ref, o_ref, *, tf):
    del be_ref
    used = pl.program_id(0) < nu_ref[0]

    @pl.when(used)
    def _():
        x = x_ref[...].astype(BF16)
        for f in range(FFN_DIM // tf):
            fs = slice(tf * f, tf * (f + 1))
            a = jnp.dot(x, wg_ref[0, :, fs], preferred_element_type=F32)
            b = jnp.dot(x, wu_ref[0, :, fs], preferred_element_type=F32)
            act = (a * _sigmoid(a) * b).astype(BF16)
            y = jnp.dot(act, wd_ref[0, fs, :], preferred_element_type=F32)
            if f == 0:
                o_ref[...] = y
            else:
                o_ref[...] += y

    @pl.when(jnp.logical_not(used))
    def _():
        o_ref[...] = jnp.zeros_like(o_ref)


def _experts(block_expert, n_used, xpad, wg, wu, wd, *, tf):
    n_rows = xpad.shape[0]
    n_blocks = n_rows // MOE_BLOCK
    wspec = lambda s: pl.BlockSpec((1,) + s, lambda i, be, nu: (be[i], 0, 0),
                                   pipeline_mode=pl.Buffered(1))
    grid_spec = pltpu.PrefetchScalarGridSpec(
        num_scalar_prefetch=2,
        grid=(n_blocks,),
        in_specs=[pl.BlockSpec((MOE_BLOCK, D_MODEL), lambda i, be, nu: (i, 0)),
                  wspec((D_MODEL, FFN_DIM)), wspec((D_MODEL, FFN_DIM)),
                  wspec((FFN_DIM, D_MODEL))],
        out_specs=pl.BlockSpec((MOE_BLOCK, D_MODEL), lambda i, be, nu: (i, 0)),
    )
    return pl.pallas_call(
        functools.partial(_expert_kernel, tf=tf),
        grid_spec=grid_spec,
        out_shape=jax.ShapeDtypeStruct((n_rows, D_MODEL), F32),
        compiler_params=_cparams(("arbitrary",)),
        name="experts",
    )(block_expert, n_used, xpad, wg, wu, wd)


def _combine_kernel(pos_hbm, wts_ref, x_ref, ypad_hbm, ng_ref, o_ref, pos_smem, ybuf, sem_idx,
                    sem_rows, *, tm):
    i = pl.program_id(0)
    idx_cp = pltpu.make_async_copy(pos_hbm.at[pl.ds(pl.multiple_of(i * (2 * tm), 2 * tm), 2 * tm)],
                                   pos_smem, sem_idx)
    idx_cp.start()
    idx_cp.wait()

    def issue(r, carry):
        _row_copy(ypad_hbm, pos_smem[2 * r], ybuf.at[0], r, sem_rows).start()
        _row_copy(ypad_hbm, pos_smem[2 * r + 1], ybuf.at[1], r, sem_rows).start()
        return carry

    lax.fori_loop(0, tm, issue, 0)

    def drain(r, carry):
        _row_copy(ypad_hbm, 0, ybuf.at[0], 0, sem_rows).wait()
        _row_copy(ypad_hbm, 0, ybuf.at[1], 0, sem_rows).wait()
        return carry

    lax.fori_loop(0, tm, drain, 0)

    w = wts_ref[...]
    f = w[:, 0:1] * ybuf[0] + w[:, 1:2] * ybuf[1]
    xn = x_ref[...] + f
    ms = jnp.mean(xn * xn, axis=-1, keepdims=True)
    o_ref[...] = xn * lax.rsqrt(ms + EPS) * ng_ref[...]


def _combine(pos_flat, wts, x2, ypad, final_g, *, tm):
    n = x2.shape[0]
    return pl.pallas_call(
        functools.partial(_combine_kernel, tm=tm),
        grid=(n // tm,),
        in_specs=[pl.BlockSpec(memory_space=pl.ANY),
                  pl.BlockSpec((tm, LANES), lambda i: (i, 0)),
                  pl.BlockSpec((tm, D_MODEL), lambda i: (i, 0)),
                  pl.BlockSpec(memory_space=pl.ANY),
                  pl.BlockSpec((1, D_MODEL), lambda i: (0, 0))],
        out_specs=pl.BlockSpec((tm, D_MODEL), lambda i: (i, 0)),
        out_shape=jax.ShapeDtypeStruct((n, D_MODEL), F32),
        scratch_shapes=[pltpu.SMEM((2 * tm,), jnp.int32), pltpu.VMEM((2, tm, D_MODEL), F32),
                        pltpu.SemaphoreType.DMA, pltpu.SemaphoreType.DMA],
        compiler_params=_cparams(("arbitrary",)),
        name="combine",
    )(pos_flat, wts, x2, ypad, final_g)


def _head_perm_cols(order):
    return np.concatenate([np.arange(HEAD_DIM * h, HEAD_DIM * (h + 1)) for h in order])


def _prep_w_in(w):
    seg = lambda name: w[:, _REF_SPLITS[name][0]:_REF_SPLITS[name][0] + _REF_SPLITS[name][1]]
    scale = HEAD_DIM ** -0.5
    parts = {name: seg(name) for name in _REF_SPLITS}
    parts["qa"] = parts["qa"][:, _head_perm_cols(_SWA_HEAD_ORDER)] * scale
    parts["qc"] = parts["qc"] * scale
    order = sorted(_COL, key=_COL.get)
    cols = jnp.concatenate([parts[name] for name in order], axis=1)
    cols = jnp.pad(cols, ((0, 0), (0, PROJ_WIDTH - cols.shape[1])))
    return cols.astype(BF16)


def kernel(x, mix_norm_g, w_in, attn_sinks, gla_gate_up, gla_gate_bias, gla_norm_g, rel_bias, w_branch, w_out, ffn_norm_g, dense_w_gate, dense_w_up, dense_w_down, router_w, moe_w_gate, moe_w_up, moe_w_down, final_norm_g):
    batch, seq_len, _ = x.shape
    depth = w_in.shape[0]
    assert depth == 2, "layer 0 dense, layer 1 routed; the final norm is fused into the routed tail"
    n = batch * seq_len
    x2 = x.reshape(n, D_MODEL)

    tq = 256
    swa_segs = (SWA_PREV_CHUNKS * CHUNK, tq)
    rel_segs = (tq, tq, tq)
    assert seq_len % tq == 0 and (REL_PREV_CHUNKS * CHUNK) == 2 * tq
    tm_proj = min(1024, n)
    tm_merge = min(512, n)
    tm_ffn = min(1024, n)
    tm_route = 512
    tm_gather = 512

    slopes = 2.0 ** (-8.0 * jnp.arange(1, SWA_HEADS + 1, dtype=F32) / SWA_HEADS)
    slopes = slopes[jnp.asarray(_SWA_HEAD_ORDER)]
    swa_bias = _band_bias(lambda d: -slopes[:, None, None] * jnp.abs(d).astype(F32)[None],
                          tq, swa_segs, SWA_PREV_CHUNKS)
    r = np.arange(tq)
    ltri = jnp.asarray((r[None, :] <= r[:, None]) & (r[None, :] // CHUNK == r[:, None] // CHUNK), BF16)
    lfull = jnp.asarray(r[None, :] // CHUNK == r[:, None] // CHUNK, BF16)
    rr = np.arange(tm_route)
    lstrict = jnp.asarray(rr[None, :] < rr[:, None], BF16)
    swa_perm_rows = _head_perm_cols(_SWA_HEAD_ORDER)

    for layer in range(depth):
        w_proj = _prep_w_in(w_in[layer])
        proj = _inproj(x2, mix_norm_g[layer][None], w_proj, tm_proj)

        sinks = jnp.broadcast_to(attn_sinks[layer][jnp.asarray(_SWA_HEAD_ORDER)][:, None].astype(F32),
                                 (SWA_HEADS, LANES))
        o_a = _banded_attention(proj, swa_bias, sinks, q_col=_COL["qa"], k_col=_COL["ka"],
                                v_col=_COL["va"], n_heads=SWA_HEADS, kv_shared=True,
                                seg_lens=swa_segs, tq=tq, seq_len=seq_len)

        gu = jnp.pad(gla_gate_up[layer], ((0, LANES - GLA_RANK), (0, 0))).astype(BF16)
        o_b = _gla(proj, gu, gla_gate_bias[layer][None].astype(F32), gla_norm_g[layer][None].astype(F32),
                   ltri, lfull, tq=tq, batch=batch, seq_len=seq_len)

        rb = rel_bias[layer].astype(F32)
        rel_b = _band_bias(lambda d: rb[:, jnp.clip(d, -REL_MAX_DIST, REL_MAX_DIST) + REL_MAX_DIST],
                           tq, rel_segs, REL_PREV_CHUNKS)
        o_c = _banded_attention(proj, rel_b, None, q_col=_COL["qc"], k_col=_COL["kc"],
                                v_col=_COL["vc"], n_heads=REL_HEADS, kv_shared=False,
                                seg_lens=rel_segs, tq=tq, seq_len=seq_len)

        wb = w_branch[layer]
        aw = SWA_HEADS * HEAD_DIM
        bw = GLA_HEADS * GLA_DV
        wa = wb[:aw][swa_perm_rows].astype(BF16)
        wbb = wb[aw:aw + bw].astype(BF16)
        wc = wb[aw + bw:].astype(BF16)
        wo = w_out[layer].astype(BF16)
        ng = ffn_norm_g[layer][None].astype(F32)
        li = layer // 2
        if layer % 2 == 0:
            x_mid, hn = _merge(proj, o_a, o_b, o_c, x2, wa, wbb, wc, wo, ng, None, tm=tm_merge)
            x2 = _dense_ffn(hn, x_mid, dense_w_gate[li].astype(BF16), dense_w_up[li].astype(BF16),
                            dense_w_down[li].astype(BF16), tm=tm_ffn, tf=512)
        else:
            rw = jnp.pad(router_w[li], ((0, 0), (0, LANES - N_EXPERTS))).astype(BF16)
            x_mid, hn32, logits = _merge(proj, o_a, o_b, o_c, x2, wa, wbb, wc, wo, ng, rw, tm=tm_merge)
            meta, wts, cnt = _route(logits, lstrict, tm=tm_route)
            counts = cnt[0, :N_EXPERTS].astype(jnp.int32)
            padded = ((counts + MOE_BLOCK - 1) // MOE_BLOCK) * MOE_BLOCK
            pend = jnp.cumsum(padded)
            pstart = pend - padded
            pos = pstart[meta[:, 0:2]] + meta[:, 2:4]
            pos_flat = pos.reshape(-1).astype(jnp.int32)
            n_blocks = -(-(2 * n) // MOE_BLOCK) + N_EXPERTS
            block_start = jnp.arange(n_blocks, dtype=jnp.int32) * MOE_BLOCK
            block_expert = jnp.minimum(jnp.sum(block_start[:, None] >= pend[None, :], axis=1),
                                       N_EXPERTS - 1).astype(jnp.int32)
            n_used = (pend[-1:] // MOE_BLOCK).astype(jnp.int32)
            xpad = jnp.zeros((n_blocks * MOE_BLOCK, D_MODEL), F32)
            xpad = _dispatch(pos_flat, hn32, xpad, tm=tm_gather)
            ypad = _experts(block_expert, n_used, xpad, moe_w_gate[li].astype(BF16),
                            moe_w_up[li].astype(BF16), moe_w_down[li].astype(BF16), tf=512)
            x2 = _combine(pos_flat, wts, x_mid, ypad, final_norm_g[None].astype(F32), tm=tm_gather)
    return x2.reshape(batch, seq_len, D_MODEL)
```
